```python
import jax, jax.numpy as jnp
from jax import lax
import numpy as np

D_MODEL = 4096
BATCH = 32
SEQ = 256
DEPTH = 1
DEC_BATCH = 8
DEC_SEQ = 1024
PAST_LEN = 512

GRID_W = 64
N_HEADS = 16
QK_NOPE = 128
QK_ROPE = 64
V_DIM = 128
QK_DIM = QK_NOPE + QK_ROPE
Q_LORA = 1024
KV_LORA = 512
ROPE_PAIRS_PER_AXIS = QK_ROPE // 4
ROPE_THETA = 10000.0
SOFTMAX_SCALE = QK_DIM ** -0.5
Q_BLOCK = 128
POOL_WIDTH = D_MODEL // 2
POOL_WINDOWS = (2, 4, 8, 16)
N_POOL_GROUPS = len(POOL_WINDOWS)
POOL_GW = POOL_WIDTH // N_POOL_GROUPS
D_FF = 128 * ((8 * D_MODEL // 3 + 127) // 128)
N_MOD = 9
EPS = 1e-6
OFF_Q = POOL_WIDTH
OFF_KV = OFF_Q + Q_LORA
OFF_KPE = OFF_KV + KV_LORA
OFF_GP = OFF_KPE + QK_ROPE
OFF_GA = OFF_GP + D_MODEL
IN_COLS = OFF_GA + D_MODEL

kernel_name = "hybrid_pool_mla_diffusion_step"


def _rmsnorm(x, g):
    xf = x.astype(jnp.float32)
    y = xf * lax.rsqrt(jnp.mean(xf * xf, axis=-1, keepdims=True) + EPS)
    return y.astype(x.dtype) * g


def _swiglu(h, w_in, w_out):
    gu = h @ w_in
    return (jax.nn.silu(gu[..., :D_FF]) * gu[..., D_FF:]) @ w_out


def _axial_rope_tables(rows):
    t = jnp.arange(rows * GRID_W)
    row = (t // GRID_W).astype(jnp.float32)
    col = (t % GRID_W).astype(jnp.float32)
    inv = ROPE_THETA ** (-jnp.arange(ROPE_PAIRS_PER_AXIS, dtype=jnp.float32) / ROPE_PAIRS_PER_AXIS)
    ang = jnp.concatenate([row[:, None] * inv, col[:, None] * inv], axis=-1)
    return jnp.cos(ang), jnp.sin(ang)


def _apply_rope(x, cos, sin):
    half = QK_ROPE // 2
    x1, x2 = x[..., :half], x[..., half:]
    cos = cos.astype(x.dtype)
    sin = sin.astype(x.dtype)
    return jnp.concatenate([x1 * cos - x2 * sin, x1 * sin + x2 * cos], axis=-1)


def _multiscale_pool(u, w_grp, scale):
    B, L, _ = u.shape
    ug = u.reshape(B, L, N_POOL_GROUPS, POOL_GW)
    cs = jnp.concatenate([jnp.zeros((B, 1, N_POOL_GROUPS, POOL_GW), jnp.float32),
                          jnp.cumsum(ug.astype(jnp.float32), axis=1)], axis=1)
    t = jnp.arange(L)
    outs = []
    for g, w in enumerate(POOL_WINDOWS):
        lo = jnp.clip(t - w // 2, 0, L)
        hi = jnp.clip(t + w // 2, 0, L)
        csg = cs[:, :, g]
        win_sum = jnp.take(csg, hi, axis=1) - jnp.take(csg, lo, axis=1)
        cnt = (hi - lo).astype(jnp.float32)[None, :, None]
        outs.append(win_sum / cnt - ug[:, :, g].astype(jnp.float32))
    pooled = jnp.stack(outs, axis=2).astype(u.dtype)
    mixed = jnp.einsum('blgc,gcd->blgd', pooled, w_grp)
    return mixed.reshape(B, L, POOL_WIDTH) * scale


def _mixer_inputs(h, w_in, g_qa, w_qb, g_kva):
    z = h @ w_in
    u = z[..., :OFF_Q]
    q_lat = z[..., OFF_Q:OFF_KV]
    kv_lat = z[..., OFF_KV:OFF_KPE]
    k_pe = z[..., OFF_KPE:OFF_GP]
    gate_pool = z[..., OFF_GP:OFF_GA]
    gate_attn = z[..., OFF_GA:]
    q = (_rmsnorm(q_lat, g_qa) @ w_qb).reshape(*h.shape[:-1], N_HEADS, QK_DIM)
    ckv = _rmsnorm(kv_lat, g_kva)
    return u, q[..., :QK_NOPE], q[..., QK_NOPE:], ckv, k_pe, gate_pool, gate_attn


def _expand_kv(ckv, w_kvb):
    B, L, _ = ckv.shape
    kv = (ckv @ w_kvb).reshape(B, L, N_HEADS, QK_NOPE + V_DIM)
    return kv[..., :QK_NOPE], kv[..., QK_NOPE:]


def _attend(q_nope, q_pe, k_nope, k_pe, v):
    B, S, H, _ = q_nope.shape
    nblk = S // Q_BLOCK
    qn = q_nope.reshape(B, nblk, Q_BLOCK, H, QK_NOPE).swapaxes(0, 1)
    qp = q_pe.reshape(B, nblk, Q_BLOCK, H, QK_ROPE).swapaxes(0, 1)

    def block(args):
        bn, bp = args
        s = jnp.einsum('bqhd,bkhd->bhqk', bn, k_nope) + jnp.einsum('bqhd,bkd->bhqk', bp, k_pe)
        p = jax.nn.softmax(s.astype(jnp.float32) * SOFTMAX_SCALE, axis=-1).astype(v.dtype)
        return jnp.einsum('bhqk,bkhd->bqhd', p, v)

    o = lax.map(block, (qn, qp))
    return o.swapaxes(0, 1).reshape(B, S, H, V_DIM)


def _layer(x, mod, lw, rope=None, ctx_ckv=None, ctx_kpe=None):
    (g_pre, g_post, w_f1_in, w_f1_out, w_f2_in, w_f2_out, w_in, g_qa, w_qb, g_kva, w_kvb,
     w_pool_grp, pool_scale, w_pool_out, w_attn_out, w_o) = lw

    def pre(y, k):
        return _rmsnorm(y, g_pre[k]) * (1 + mod[:, :, 3 * k + 1]) + mod[:, :, 3 * k]

    def post(y, k):
        return mod[:, :, 3 * k + 2] * _rmsnorm(y, g_post[k])

    x = x + 0.5 * post(_swiglu(pre(x, 0), w_f1_in, w_f1_out), 0)
    h = pre(x, 1)
    u, q_nope, q_pe, ckv, k_pe, gate_pool, gate_attn = _mixer_inputs(h, w_in, g_qa, w_qb, g_kva)
    if ctx_ckv is None:
        keys_ckv, keys_pe = ckv, k_pe
    else:
        cos, sin = rope
        q_pe = _apply_rope(q_pe, cos[:, None], sin[:, None])
        keys_ckv = jnp.concatenate([ctx_ckv, ckv], axis=1)
        keys_pe = jnp.concatenate([ctx_kpe, _apply_rope(k_pe, cos, sin)], axis=1)
    k_nope, v = _expand_kv(keys_ckv, w_kvb)
    o = _attend(q_nope, q_pe, k_nope, keys_pe, v)
    y_pool = _multiscale_pool(u, w_pool_grp, pool_scale) @ w_pool_out
    y_attn = o.reshape(o.shape[0], o.shape[1], N_HEADS * V_DIM) @ w_attn_out
    y = (jax.nn.sigmoid(gate_pool) * y_pool + jax.nn.sigmoid(gate_attn) * y_attn) @ w_o
    x = x + post(y, 1)
    x = x + 0.5 * post(_swiglu(pre(x, 2), w_f2_in, w_f2_out), 2)
    return x, ckv, k_pe


def setup_inputs(seed: int = 0) -> dict:
    key = jax.random.key(seed)
    ks = jax.random.split(key, 32)
    f32 = jnp.float32

    def nrm(k, shape, scale):
        return jax.random.normal(k, shape, f32) * scale

    D = D_MODEL
    return {
        "x_prompt": nrm(ks[0], (BATCH, SEQ, D), 1.0),
        "x_sample": nrm(ks[1], (DEC_BATCH, DEC_SEQ, D), 1.0),
        "cache_ckv": nrm(ks[2], (DEC_BATCH, DEPTH, PAST_LEN, KV_LORA), 1.0),
        "cache_kpe": nrm(ks[3], (DEC_BATCH, DEPTH, PAST_LEN, QK_ROPE), 1.0),
        "c": nrm(ks[4], (DEC_BATCH, D), 1.0),
        "c_ctx": nrm(ks[5], (D,), 1.0),
        "w_mod": nrm(ks[6], (DEPTH, D, N_MOD * D), 0.5 * D ** -0.5),
        "b_mod": nrm(ks[7], (DEPTH, N_MOD * D), 0.01),
        "g_pre": 1.0 + nrm(ks[8], (DEPTH, 3, D), 0.05),
        "g_post": 1.0 + nrm(ks[9], (DEPTH, 3, D), 0.05),
        "w_ffn1_in": nrm(ks[10], (DEPTH, D, 2 * D_FF), D ** -0.5),
        "w_ffn1_out": nrm(ks[11], (DEPTH, D_FF, D), D_FF ** -0.5),
        "w_ffn2_in": nrm(ks[12], (DEPTH, D, 2 * D_FF), D ** -0.5),
        "w_ffn2_out": nrm(ks[13], (DEPTH, D_FF, D), D_FF ** -0.5),
        "w_in": nrm(ks[14], (DEPTH, D, IN_COLS), D ** -0.5),
        "g_qa": 1.0 + nrm(ks[15], (DEPTH, Q_LORA), 0.05),
        "w_qb": nrm(ks[16], (DEPTH, Q_LORA, N_HEADS * QK_DIM), Q_LORA ** -0.5),
        "g_kva": 1.0 + nrm(ks[17], (DEPTH, KV_LORA), 0.05),
        "w_kvb": nrm(ks[18], (DEPTH, KV_LORA, N_HEADS * (QK_NOPE + V_DIM)), KV_LORA ** -0.5),
        "w_pool_grp": nrm(ks[19], (DEPTH, N_POOL_GROUPS, POOL_GW, POOL_GW), POOL_GW ** -0.5),
        "pool_scale": 1.0 + nrm(ks[20], (DEPTH, POOL_WIDTH), 0.1),
        "w_pool_out": nrm(ks[21], (DEPTH, POOL_WIDTH, D), POOL_WIDTH ** -0.5),
        "w_attn_out": nrm(ks[22], (DEPTH, N_HEADS * V_DIM, D), (N_HEADS * V_DIM) ** -0.5),
        "w_o": nrm(ks[23], (DEPTH, D, D), D ** -0.5),
    }


def reference(x_prompt, x_sample, cache_ckv, cache_kpe, c, c_ctx, w_mod, b_mod, g_pre, g_post,
              w_ffn1_in, w_ffn1_out, w_ffn2_in, w_ffn2_out, w_in, g_qa, w_qb, g_kva, w_kvb,
              w_pool_grp, pool_scale, w_pool_out, w_attn_out, w_o):
    rows = x_sample.shape[1] // GRID_W
    rope = _axial_rope_tables(rows)
    xp, xs = x_prompt, x_sample
    new_ckv, new_kpe = [], []
    for l in range(DEPTH):
        lw = (g_pre[l], g_post[l], w_ffn1_in[l], w_ffn1_out[l], w_ffn2_in[l], w_ffn2_out[l],
              w_in[l], g_qa[l], w_qb[l], g_kva[l], w_kvb[l], w_pool_grp[l], pool_scale[l],
              w_pool_out[l], w_attn_out[l], w_o[l])
        mod_ctx = (jax.nn.silu(c_ctx[None]) @ w_mod[l] + b_mod[l]).reshape(1, 1, N_MOD, D_MODEL)
        mod_lat = (jax.nn.silu(c) @ w_mod[l] + b_mod[l]).reshape(c.shape[0], 1, N_MOD, D_MODEL)
        xp, ckv_l, kpe_l = _layer(xp, mod_ctx, lw)
        new_ckv.append(ckv_l)
        new_kpe.append(kpe_l)
        xs, _, _ = _layer(xs, mod_lat, lw, rope, cache_ckv[:, l], cache_kpe[:, l])
    state_ckv = jnp.stack(new_ckv, axis=1)
    state_kpe = jnp.stack(new_kpe, axis=1)
    return (xp, xs, state_ckv, state_kpe)
```

```python
import functools

import jax
import jax.numpy as jnp
from jax import lax
from jax.experimental import pallas as pl
from jax.experimental.pallas import tpu as pltpu

F32 = jnp.float32
BF16 = jnp.bfloat16

D_MODEL = 4096
GRID_W = 64
N_HEADS = 16
QK_NOPE = 128
QK_ROPE = 64
V_DIM = 128
QK_DIM = QK_NOPE + QK_ROPE
Q_LORA = 1024
KV_LORA = 512
ROPE_PAIRS_PER_AXIS = QK_ROPE // 4
ROPE_THETA = 10000.0
SOFTMAX_SCALE = QK_DIM ** -0.5
POOL_WIDTH = D_MODEL // 2
POOL_WINDOWS = (2, 4, 8, 16)
N_POOL_GROUPS = len(POOL_WINDOWS)
POOL_GW = POOL_WIDTH // N_POOL_GROUPS
D_FF = 128 * ((8 * D_MODEL // 3 + 127) // 128)
N_MOD = 9
EPS = 1e-6
OFF_Q = POOL_WIDTH
OFF_KV = OFF_Q + Q_LORA
OFF_KPE = OFF_KV + KV_LORA
OFF_GP = OFF_KPE + QK_ROPE
OFF_GA = OFF_GP + D_MODEL

LANES = 128
HEAD_PAIR = 2
MOD_ROWS = 16
QKV_COLS = Q_LORA + KV_LORA + LANES
VMEM_LIMIT_BYTES = 56 * 1024 * 1024


def _params(*semantics):
    return pltpu.CompilerParams(dimension_semantics=semantics, vmem_limit_bytes=VMEM_LIMIT_BYTES)


def _rms(x, g):
    return x * lax.rsqrt(jnp.mean(x * x, axis=-1, keepdims=True) + EPS) * g


def _pre(x, g, scale, shift):
    return _rms(x, g) * (1.0 + scale) + shift


def _mod_kernel(c_ref, w_ref, b_ref, o_ref):
    c = c_ref[...]
    s = (c * jax.nn.sigmoid(c)).astype(BF16)
    o_ref[...] = jnp.dot(s, w_ref[...].astype(BF16), preferred_element_type=F32) + b_ref[...]


def _modulation(cc, w_mod, b_mod, tn=512):
    n = w_mod.shape[1]
    return pl.pallas_call(
        _mod_kernel,
        grid=(n // tn,),
        in_specs=[pl.BlockSpec((MOD_ROWS, D_MODEL), lambda j: (0, 0)),
                  pl.BlockSpec((D_MODEL, tn), lambda j: (0, j)),
                  pl.BlockSpec((1, tn), lambda j: (0, j))],
        out_specs=pl.BlockSpec((MOD_ROWS, tn), lambda j: (0, j)),
        out_shape=jax.ShapeDtypeStruct((MOD_ROWS, n), F32),
        compiler_params=_params("parallel"),
        name="modulation",
    )(cc, w_mod, b_mod)


def _pre_kernel(k, x_ref, mod_ref, g_ref, h_ref):
    h_ref[...] = _pre(x_ref[...], g_ref[k:k + 1, :], mod_ref[3 * k + 1:3 * k + 2, :],
                      mod_ref[3 * k:3 * k + 1, :]).astype(BF16)


def _prenorm(x, mod, g_pre, k, rows_per_group, tm=256):
    m = x.shape[0]
    return pl.pallas_call(
        functools.partial(_pre_kernel, k),
        grid=(m // tm,),
        in_specs=[pl.BlockSpec((tm, D_MODEL), lambda i: (i, 0)),
                  pl.BlockSpec((None, N_MOD, D_MODEL), lambda i: ((i * tm) // rows_per_group, 0, 0)),
                  pl.BlockSpec((3, D_MODEL), lambda i: (0, 0))],
        out_specs=pl.BlockSpec((tm, D_MODEL), lambda i: (i, 0)),
        out_shape=jax.ShapeDtypeStruct((m, D_MODEL), BF16),
        compiler_params=_params("parallel"),
        name="prenorm",
    )(x, mod, g_pre)


def _post_kernel(k, step, with_next, y_ref, x_ref, mod_ref, gpost_ref, gpre_ref, xo_ref, *h_ref):
    gate = mod_ref[3 * k + 2:3 * k + 3, :]
    xn = x_ref[...] + step * (gate * _rms(y_ref[...], gpost_ref[k:k + 1, :]))
    xo_ref[...] = xn
    if with_next:
        kn = k + 1
        h_ref[0][...] = _pre(xn, gpre_ref[kn:kn + 1, :], mod_ref[3 * kn + 1:3 * kn + 2, :],
                             mod_ref[3 * kn:3 * kn + 1, :]).astype(BF16)


def _postnorm(y, x, mod, g_post, g_pre, k, step, with_next, rows_per_group, tm=256):
    m = x.shape[0]
    row = pl.BlockSpec((tm, D_MODEL), lambda i: (i, 0))
    out_shape = [jax.ShapeDtypeStruct((m, D_MODEL), F32)]
    out_specs = [row]
    if with_next:
        out_shape.append(jax.ShapeDtypeStruct((m, D_MODEL), BF16))
        out_specs.append(row)
    return pl.pallas_call(
        functools.partial(_post_kernel, k, step, with_next),
        grid=(m // tm,),
        in_specs=[row, row,
                  pl.BlockSpec((None, N_MOD, D_MODEL), lambda i: ((i * tm) // rows_per_group, 0, 0)),
                  pl.BlockSpec((3, D_MODEL), lambda i: (0, 0)),
                  pl.BlockSpec((3, D_MODEL), lambda i: (0, 0))],
        out_specs=out_specs,
        out_shape=out_shape,
        compiler_params=_params("parallel"),
        name="postnorm",
    )(y, x, mod, g_post, g_pre)


def _ffn_in_kernel(h_ref, wg_ref, wu_ref, o_ref):
    h = h_ref[...]
    g = jnp.dot(h, wg_ref[...], preferred_element_type=F32)
    u = jnp.dot(h, wu_ref[...], preferred_element_type=F32)
    o_ref[...] = (g * jax.nn.sigmoid(g) * u).astype(BF16)


def _ffn_in(h, w_in, tm=1024, tn=256):
    m = h.shape[0]
    nblk = D_FF // tn
    return pl.pallas_call(
        _ffn_in_kernel,
        grid=(m // tm, nblk),
        in_specs=[pl.BlockSpec((tm, D_MODEL), lambda i, j: (i, 0)),
                  pl.BlockSpec((D_MODEL, tn), lambda i, j: (0, j)),
                  pl.BlockSpec((D_MODEL, tn), lambda i, j: (0, j + nblk))],
        out_specs=pl.BlockSpec((tm, tn), lambda i, j: (i, j)),
        out_shape=jax.ShapeDtypeStruct((m, D_FF), BF16),
        compiler_params=_params("parallel", "parallel"),
        name="ffn_in",
    )(h, w_in, w_in)


def _mm_kernel(sigmoid, a_ref, b_ref, o_ref):
    r = jnp.dot(a_ref[...], b_ref[...], preferred_element_type=F32)
    if sigmoid:
        r = jax.nn.sigmoid(r)
    o_ref[...] = r.astype(o_ref.dtype)


def _matmul(a, b, out_dtype, tm, tn, sigmoid=False, name="matmul"):
    m, k = a.shape
    n = b.shape[1]
    return pl.pallas_call(
        functools.partial(_mm_kernel, sigmoid),
        grid=(m // tm, n // tn),
        in_specs=[pl.BlockSpec((tm, k), lambda i, j: (i, 0)),
                  pl.BlockSpec((k, tn), lambda i, j: (0, j))],
        out_specs=pl.BlockSpec((tm, tn), lambda i, j: (i, j)),
        out_shape=jax.ShapeDtypeStruct((m, n), out_dtype),
        compiler_params=_params("parallel", "parallel"),
        name=name,
    )(a, b)


def _window_pool(u, pos, window, seq_len):
    rows = u.shape[0]
    half = window // 2

    def down(x, s):
        return jnp.where(pos >= s, pltpu.roll(x, s, 0), 0.0)

    def up(x, s):
        return jnp.where(pos < seq_len - s, pltpu.roll(x, rows - s, 0), 0.0)

    left = u
    right = u
    width = 1
    while width < half:
        left = left + down(left, width)
        right = right + up(right, width)
        width *= 2
    total = down(left, 1) + right
    cnt = jnp.minimum(pos + half, seq_len) - jnp.maximum(pos - half, 0)
    return total * (1.0 / cnt.astype(F32)) - u


def _pool_kernel(seq_len, h_ref, w_ref, wgrp_ref, scale_ref, o_ref, pooled_ref):
    grp = pl.program_id(1)
    u = jnp.dot(h_ref[...], w_ref[...], preferred_element_type=F32)
    pos = lax.broadcasted_iota(jnp.int32, (u.shape[0], 1), 0) & (seq_len - 1)
    for gi, window in enumerate(POOL_WINDOWS):
        @pl.when(grp == gi)
        def _(window=window):
            pooled_ref[...] = _window_pool(u, pos, window, seq_len).astype(BF16)
    mixed = jnp.dot(pooled_ref[...], wgrp_ref[...], preferred_element_type=F32)
    o_ref[...] = (mixed * scale_ref[...]).astype(BF16)


def _pool_mixer(h, w_u, w_grp, pool_scale, seq_len, tm=1024):
    m = h.shape[0]
    assert tm % seq_len == 0 and seq_len & (seq_len - 1) == 0
    return pl.pallas_call(
        functools.partial(_pool_kernel, seq_len),
        grid=(m // tm, N_POOL_GROUPS),
        in_specs=[pl.BlockSpec((tm, D_MODEL), lambda i, g: (i, 0)),
                  pl.BlockSpec((D_MODEL, POOL_GW), lambda i, g: (0, g)),
                  pl.BlockSpec((None, POOL_GW, POOL_GW), lambda i, g: (g, 0, 0)),
                  pl.BlockSpec((1, POOL_GW), lambda i, g: (0, g))],
        out_specs=pl.BlockSpec((tm, POOL_GW), lambda i, g: (i, g)),
        out_shape=jax.ShapeDtypeStruct((m, POOL_WIDTH), BF16),
        scratch_shapes=[pltpu.VMEM((tm, POOL_GW), BF16)],
        compiler_params=_params("parallel", "arbitrary"),
        name="pool_mixer",
    )(h, w_u, w_grp, pool_scale)


def _rope(x, cos, sin_lo, sin_hi):
    return (x * cos + pltpu.roll(x, LANES - QK_ROPE // 2, 1) * sin_lo
            + pltpu.roll(x, QK_ROPE // 2, 1) * sin_hi)


def _qkv_kernel(rope, h_ref, w_ref, gqa_ref, gkva_ref, wqb_ref, *refs):
    if rope:
        cos_ref, slo_ref, shi_ref, q_ref, ckv_ref, kpe_ref = refs
    else:
        q_ref, ckv_ref, kpe_ref, ckv32_ref, kpe32_ref = refs
    z = jnp.dot(h_ref[...], w_ref[...], preferred_element_type=F32)
    qn = _rms(z[:, :Q_LORA], gqa_ref[...]).astype(BF16)
    ckv = _rms(z[:, Q_LORA:Q_LORA + KV_LORA], gkva_ref[...])
    kpe = z[:, Q_LORA + KV_LORA:]
    q = jnp.dot(qn, wqb_ref[...], preferred_element_type=F32)
    nope_cols = N_HEADS * QK_NOPE
    q_ref[:, :nope_cols] = q[:, :nope_cols].astype(BF16)
    if rope:
        cos, slo, shi = cos_ref[...], slo_ref[...], shi_ref[...]
        for c in range(N_HEADS * QK_ROPE // LANES):
            lo = nope_cols + c * LANES
            q_ref[:, lo:lo + LANES] = _rope(q[:, lo:lo + LANES], cos, slo, shi).astype(BF16)
        kpe_ref[...] = _rope(kpe, cos, slo, shi).astype(BF16)
    else:
        q_ref[:, nope_cols:] = q[:, nope_cols:].astype(BF16)
        kpe_ref[...] = kpe.astype(BF16)
        ckv32_ref[...] = ckv
        kpe32_ref[...] = kpe[:, :QK_ROPE]
    ckv_ref[...] = ckv.astype(BF16)


def _qkv(h, w_qkv, g_qa, g_kva, w_qb, rope_tables, seq_len, tm=512):
    m = h.shape[0]
    q_cols = N_HEADS * QK_DIM
    const = lambda i: (0, 0)
    resident = pl.Buffered(1)
    in_specs = [pl.BlockSpec((tm, D_MODEL), lambda i: (i, 0)),
                pl.BlockSpec((D_MODEL, QKV_COLS), const, pipeline_mode=resident),
                pl.BlockSpec((1, Q_LORA), const),
                pl.BlockSpec((1, KV_LORA), const),
                pl.BlockSpec((Q_LORA, q_cols), const, pipeline_mode=resident)]
    out_shape = [jax.ShapeDtypeStruct((m, q_cols), BF16),
                 jax.ShapeDtypeStruct((m, KV_LORA), BF16),
                 jax.ShapeDtypeStruct((m, LANES), BF16)]
    out_specs = [pl.BlockSpec((tm, q_cols), lambda i: (i, 0)),
                 pl.BlockSpec((tm, KV_LORA), lambda i: (i, 0)),
                 pl.BlockSpec((tm, LANES), lambda i: (i, 0))]
    args = [h, w_qkv, g_qa, g_kva, w_qb]
    rope = rope_tables is not None
    if rope:
        blocks_per_seq = seq_len // tm
        in_specs += [pl.BlockSpec((tm, LANES), lambda i: (i % blocks_per_seq, 0))] * 3
        args += list(rope_tables)
    else:
        out_shape += [jax.ShapeDtypeStruct((m, KV_LORA), F32), jax.ShapeDtypeStruct((m, QK_ROPE), F32)]
        out_specs += [pl.BlockSpec((tm, KV_LORA), lambda i: (i, 0)),
                      pl.BlockSpec((tm, QK_ROPE), lambda i: (i, 0))]
    return pl.pallas_call(
        functools.partial(_qkv_kernel, rope),
        grid=(m // tm,),
        in_specs=in_specs,
        out_specs=out_specs,
        out_shape=out_shape,
        compiler_params=_params("parallel"),
        name="qkv_rope" if rope else "qkv",
    )(*args)


def _attn_kernel(qn_ref, qp_ref, kv_ref, kpe_ref, o_ref):
    qp = qp_ref[...]
    kpe = kpe_ref[...]
    lane = lax.broadcasted_iota(jnp.int32, qp.shape, 1)
    for hh in range(HEAD_PAIR):
        kv_lo = hh * (QK_NOPE + V_DIM)
        in_head = (lane >= QK_ROPE) if hh else (lane < QK_ROPE)
        qc = jnp.concatenate([qn_ref[:, hh * QK_NOPE:(hh + 1) * QK_NOPE],
                              jnp.where(in_head, qp, jnp.zeros_like(qp))], axis=1)
        kc = jnp.concatenate([kv_ref[:, kv_lo:kv_lo + QK_NOPE], kpe], axis=1)
        s = lax.dot_general(qc, kc, (((1,), (1,)), ((), ())), preferred_element_type=F32) * SOFTMAX_SCALE
        e = jnp.exp(s - jnp.max(s, axis=-1, keepdims=True))
        p = e * (1.0 / jnp.sum(e, axis=-1, keepdims=True))
        o = jnp.dot(p.astype(BF16), kv_ref[:, kv_lo + QK_NOPE:kv_lo + QK_NOPE + V_DIM],
                    preferred_element_type=F32)
        o_ref[:, hh * V_DIM:(hh + 1) * V_DIM] = o.astype(BF16)


def _attention(q, kv, kpe, batch, seq_len, n_keys, tq):
    q3 = q.reshape(batch, seq_len, N_HEADS * QK_DIM)
    kv3 = kv.reshape(batch, n_keys, N_HEADS * (QK_NOPE + V_DIM))
    kpe3 = kpe.reshape(batch, n_keys, LANES)
    rope_blk0 = N_HEADS * QK_NOPE // LANES
    pair_nope = HEAD_PAIR * QK_NOPE
    pair_kv = HEAD_PAIR * (QK_NOPE + V_DIM)
    pair_v = HEAD_PAIR * V_DIM
    out = pl.pallas_call(
        _attn_kernel,
        grid=(batch, N_HEADS // HEAD_PAIR, seq_len // tq),
        in_specs=[pl.BlockSpec((None, tq, pair_nope), lambda b, h, i: (b, i, h)),
                  pl.BlockSpec((None, tq, LANES), lambda b, h, i: (b, i, rope_blk0 + h)),
                  pl.BlockSpec((None, n_keys, pair_kv), lambda b, h, i: (b, 0, h)),
                  pl.BlockSpec((None, n_keys, LANES), lambda b, h, i: (b, 0, 0))],
        out_specs=pl.BlockSpec((None, tq, pair_v), lambda b, h, i: (b, i, h)),
        out_shape=jax.ShapeDtypeStruct((batch, seq_len, N_HEADS * V_DIM), BF16),
        compiler_params=_params("parallel", "parallel", "parallel"),
        name="attention",
    )(q3, q3, kv3, kpe3)
    return out.reshape(batch * seq_len, N_HEADS * V_DIM)


def _merge_kernel(mixed_ref, o_ref, wp_ref, wa_ref, gp_ref, ga_ref, out_ref):
    yp = jnp.dot(mixed_ref[...], wp_ref[...], preferred_element_type=F32)
    ya = jnp.dot(o_ref[...], wa_ref[...], preferred_element_type=F32)
    out_ref[...] = (gp_ref[...].astype(F32) * yp + ga_ref[...].astype(F32) * ya).astype(BF16)


def _merge(mixed, o, w_pool_out, w_attn_out, gates, tm=1024, tn=512):
    m = mixed.shape[0]
    ga_blk0 = D_MODEL // tn
    return pl.pallas_call(
        _merge_kernel,
        grid=(m // tm, D_MODEL // tn),
        in_specs=[pl.BlockSpec((tm, POOL_WIDTH), lambda i, j: (i, 0)),
                  pl.BlockSpec((tm, N_HEADS * V_DIM), lambda i, j: (i, 0)),
                  pl.BlockSpec((POOL_WIDTH, tn), lambda i, j: (0, j)),
                  pl.BlockSpec((N_HEADS * V_DIM, tn), lambda i, j: (0, j)),
                  pl.BlockSpec((tm, tn), lambda i, j: (i, j)),
                  pl.BlockSpec((tm, tn), lambda i, j: (i, ga_blk0 + j))],
        out_specs=pl.BlockSpec((tm, tn), lambda i, j: (i, j)),
        out_shape=jax.ShapeDtypeStruct((m, D_MODEL), BF16),
        compiler_params=_params("parallel", "parallel"),
        name="merge",
    )(mixed, o, w_pool_out, w_attn_out, gates, gates)


def _rope_tables(seq_len):
    t = jnp.arange(seq_len)
    row = (t // GRID_W).astype(F32)
    col = (t % GRID_W).astype(F32)
    inv = ROPE_THETA ** (-jnp.arange(ROPE_PAIRS_PER_AXIS, dtype=F32) / ROPE_PAIRS_PER_AXIS)
    ang = jnp.concatenate([row[:, None] * inv, col[:, None] * inv], axis=-1)
    cos, sin = jnp.cos(ang), jnp.sin(ang)
    zero = jnp.zeros_like(sin)
    reps = LANES // QK_ROPE
    cos_t = jnp.tile(jnp.concatenate([cos, cos], axis=-1), (1, reps))
    sin_lo = jnp.tile(jnp.concatenate([-sin, zero], axis=-1), (1, reps))
    sin_hi = jnp.tile(jnp.concatenate([zero, sin], axis=-1), (1, reps))
    return cos_t, sin_lo, sin_hi


def _layer(x, mod, rows_per_group, wts, batch, seq_len, rope_tables, ctx):
    (g_pre, g_post, w_f1_in, w_f1_out, w_f2_in, w_f2_out, w_u, w_qkv, w_gates, g_qa, w_qb, g_kva, w_kvb,
     w_grp, pool_scale, w_pool_out, w_attn_out, w_o) = wts
    post = functools.partial(_postnorm, mod=mod, g_post=g_post, g_pre=g_pre, rows_per_group=rows_per_group)

    h = _prenorm(x, mod, g_pre, 0, rows_per_group)
    y = _matmul(_ffn_in(h, w_f1_in), w_f1_out, F32, 512, 512, name="ffn_out")
    x, h = post(y, x, k=0, step=0.5, with_next=True)

    mixed = _pool_mixer(h, w_u, w_grp, pool_scale, seq_len)
    gates = _matmul(h, w_gates, BF16, 1024, 1024, sigmoid=True, name="gates")
    if ctx is None:
        q, ckv, kpe, ckv32, kpe32 = _qkv(h, w_qkv, g_qa, g_kva, w_qb, None, seq_len)
        n_keys = seq_len
        state = (ckv32, kpe32)
    else:
        q, ckv, kpe = _qkv(h, w_qkv, g_qa, g_kva, w_qb, rope_tables, seq_len)
        ctx_ckv, ctx_kpe = ctx
        n_keys = ctx_ckv.shape[1] + seq_len
        ckv = jnp.concatenate([ctx_ckv.astype(BF16), ckv.reshape(batch, seq_len, KV_LORA)], axis=1)
        ctx_kpe = jnp.concatenate([ctx_kpe, ctx_kpe], axis=-1).astype(BF16)
        kpe = jnp.concatenate([ctx_kpe, kpe.reshape(batch, seq_len, LANES)], axis=1)
        ckv = ckv.reshape(batch * n_keys, KV_LORA)
        kpe = kpe.reshape(batch * n_keys, LANES)
        state = None
    kv = _matmul(ckv, w_kvb, BF16, 1024 if ctx is None else 1536, 1024, name="kv_expand")
    o = _attention(q, kv, kpe, batch, seq_len, n_keys, tq=min(seq_len, 512))
    yg = _merge(mixed, o, w_pool_out, w_attn_out, gates)
    y = _matmul(yg, w_o, F32, 1024, 1024, name="out_proj")
    x, h = post(y, x, k=1, step=1.0, with_next=True)

    y = _matmul(_ffn_in(h, w_f2_in), w_f2_out, F32, 512, 512, name="ffn_out")
    (x,) = post(y, x, k=2, step=0.5, with_next=False)
    return x, state


def kernel(x_prompt, x_sample, cache_ckv, cache_kpe, c, c_ctx, w_mod, b_mod, g_pre, g_post, w_ffn1_in, w_ffn1_out, w_ffn2_in, w_ffn2_out, w_in, g_qa, w_qb, g_kva, w_kvb, w_pool_grp, pool_scale, w_pool_out, w_attn_out, w_o):
    batch, seq, _ = x_prompt.shape
    dec_batch, dec_seq, _ = x_sample.shape
    depth = w_mod.shape[0]
    rope_tables = _rope_tables(dec_seq)
    cc = jnp.concatenate([c_ctx[None], c, jnp.zeros((MOD_ROWS - 1 - dec_batch, D_MODEL), F32)], axis=0)

    xp = x_prompt.reshape(batch * seq, D_MODEL)
    xs = x_sample.reshape(dec_batch * dec_seq, D_MODEL)
    new_ckv, new_kpe = [], []
    for l in range(depth):
        mods = _modulation(cc, w_mod[l], b_mod[l][None]).reshape(MOD_ROWS, N_MOD, D_MODEL)
        w_in_l = w_in[l]
        w_qb_l = w_qb[l].reshape(Q_LORA, N_HEADS, QK_DIM)
        wts = (g_pre[l], g_post[l],
               w_ffn1_in[l].astype(BF16), w_ffn1_out[l].astype(BF16),
               w_ffn2_in[l].astype(BF16), w_ffn2_out[l].astype(BF16),
               w_in_l[:, :OFF_Q].astype(BF16),
               jnp.concatenate([w_in_l[:, OFF_Q:OFF_GP], w_in_l[:, OFF_KPE:OFF_GP]], axis=1).astype(BF16),
               w_in_l[:, OFF_GP:].astype(BF16),
               g_qa[l][None],
               jnp.concatenate([w_qb_l[:, :, :QK_NOPE].reshape(Q_LORA, N_HEADS * QK_NOPE),
                                w_qb_l[:, :, QK_NOPE:].reshape(Q_LORA, N_HEADS * QK_ROPE)],
                               axis=1).astype(BF16),
               g_kva[l][None], w_kvb[l].astype(BF16),
               w_pool_grp[l].astype(BF16), pool_scale[l][None],
               w_pool_out[l].astype(BF16), w_attn_out[l].astype(BF16), w_o[l].astype(BF16))
        xp, (ckv_l, kpe_l) = _layer(xp, mods[:1], batch * seq, wts, batch, seq, None, None)
        new_ckv.append(ckv_l.reshape(batch, seq, KV_LORA))
        new_kpe.append(kpe_l.reshape(batch, seq, QK_ROPE))
        xs, _ = _layer(xs, mods[1:1 + dec_batch], dec_seq, wts, dec_batch, dec_seq, rope_tables,
                       (cache_ckv[:, l], cache_kpe[:, l]))
    return (xp.reshape(batch, seq, D_MODEL), xs.reshape(dec_batch, dec_seq, D_MODEL),
            jnp.stack(new_ckv, axis=1), jnp.stack(new_kpe, axis=1))
```
